```python
import jax
import jax.numpy as jnp
import numpy as np

D_MODEL = 4096
BATCH = 1
SEQ = 16384
DEPTH = 4

CHUNK = 64
MEM_LEN = 256
D_MIX = D_MODEL
CONV_WIDTH = D_MIX // 4
ATT_WIDTH = D_MIX // 2
POOL_WIDTH = D_MIX - CONV_WIDTH - ATT_WIDTH
CONV_HEADS = 8
CONV_KSIZE = 3
ATT_HEAD_DIM = 128
ATT_HEADS = ATT_WIDTH // ATT_HEAD_DIM
LEFT_CHUNKS = 8
BAND_CHUNKS = LEFT_CHUNKS + 1
MAX_REL = 256
POOL_WINDOWS = (2, 4, 8, 16)
POOL_GROUPS = len(POOL_WINDOWS)
POOL_GROUP_WIDTH = POOL_WIDTH // POOL_GROUPS
X_HEADS = 4
X_HEAD_DIM = 256
X_WIDTH = X_HEADS * X_HEAD_DIM
D_FF = ((8 * D_MODEL + 3 * 256 - 1) // (3 * 256)) * 256
D_IN = 3 * CONV_WIDTH + 3 * ATT_WIDTH + POOL_WIDTH
SPLITS = (CONV_WIDTH, 2 * CONV_WIDTH, 3 * CONV_WIDTH,
          3 * CONV_WIDTH + ATT_WIDTH, 3 * CONV_WIDTH + 2 * ATT_WIDTH,
          3 * CONV_WIDTH + 3 * ATT_WIDTH)
RMS_EPS = 1e-6
NEG_INF = -1e30

kernel_name = 'hybrid_streaming_encoder'


def rmsnorm(x, g):
    xf = x.astype(jnp.float32)
    y = xf * jax.lax.rsqrt(jnp.mean(xf * xf, axis=-1, keepdims=True) + RMS_EPS)
    return (y * g.astype(jnp.float32)).astype(x.dtype)


def short_conv_mixer(b, c, h, w_conv):
    u = c * h
    s = u.shape[1]
    up = jnp.pad(u, ((0, 0), (CONV_KSIZE - 1, 0), (0, 0)))
    conv = up[:, 0:s] * w_conv[0]
    for k in range(1, CONV_KSIZE):
        conv = conv + up[:, k:k + s] * w_conv[k]
    return b * conv


def chunk_attention(q, k, v, rel_bias):
    bsz, s, h, dh = q.shape
    nc = s // CHUNK
    qc = q.reshape(bsz, nc, CHUNK, h, dh)
    pad = ((0, 0), (LEFT_CHUNKS, 0), (0, 0), (0, 0), (0, 0))
    kp = jnp.pad(k.reshape(bsz, nc, CHUNK, h, dh), pad)
    vp = jnp.pad(v.reshape(bsz, nc, CHUNK, h, dh), pad)
    scale = dh ** -0.5
    scores = jnp.concatenate(
        [jnp.einsum('bnqhd,bnkhd->bnhqk', qc, kp[:, j:j + nc],
                    preferred_element_type=jnp.float32) for j in range(BAND_CHUNKS)], axis=-1) * scale
    q_pos = np.arange(CHUNK)[:, None]
    k_pos = np.arange(BAND_CHUNKS * CHUNK)[None, :] - LEFT_CHUNKS * CHUNK
    rel_idx = np.clip(q_pos - k_pos, -MAX_REL, MAX_REL) + MAX_REL
    bias = rel_bias[:, rel_idx].astype(jnp.float32)
    key_chunk = np.arange(nc)[:, None] - LEFT_CHUNKS + np.repeat(np.arange(BAND_CHUNKS), CHUNK)[None, :]
    valid = jnp.asarray(key_chunk >= 0)
    scores = jnp.where(valid[None, :, None, None, :], scores + bias, NEG_INF)
    p = jax.nn.softmax(scores, axis=-1).astype(v.dtype)
    p = p.reshape(bsz, nc, h, CHUNK, BAND_CHUNKS, CHUNK)
    out = jnp.einsum('bnhqk,bnkhd->bnqhd', p[:, :, :, :, 0, :], vp[:, 0:nc])
    for j in range(1, BAND_CHUNKS):
        out = out + jnp.einsum('bnhqk,bnkhd->bnqhd', p[:, :, :, :, j, :], vp[:, j:j + nc])
    return out.reshape(bsz, s, h * dh)


def pool_mixer(u, w_pool, pool_scale):
    bsz, s, _ = u.shape
    uf = u.astype(jnp.float32)
    cs = jnp.pad(jnp.cumsum(uf, axis=1), ((0, 0), (1, 0), (0, 0)))
    t = np.arange(s)
    groups = []
    for g, w in enumerate(POOL_WINDOWS):
        lo, hi = g * POOL_GROUP_WIDTH, (g + 1) * POOL_GROUP_WIDTH
        cg = cs[:, :, lo:hi]
        start = jnp.pad(cg[:, :s - w + 1], ((0, 0), (w - 1, 0), (0, 0)))
        count = jnp.asarray(np.minimum(t + 1, w).astype(np.float32))[None, :, None]
        groups.append((cg[:, 1:] - start) / count - uf[:, :, lo:hi])
    pooled = jnp.stack(groups, axis=2).astype(u.dtype)
    mixed = jnp.einsum('bsgc,gcd->bsgd', pooled, w_pool)
    return mixed.reshape(bsz, s, POOL_WIDTH) * pool_scale


def memory_cross_attention(xn, memn, w_xq, w_xkv, w_xo):
    bsz, s, _ = xn.shape
    q = (xn @ w_xq).reshape(bsz, s, X_HEADS, X_HEAD_DIM)
    kv = (memn @ w_xkv).reshape(bsz, MEM_LEN, 2, X_HEADS, X_HEAD_DIM)
    k, v = kv[:, :, 0], kv[:, :, 1]
    scores = jnp.einsum('bshd,bmhd->bhsm', q, k, preferred_element_type=jnp.float32) * (X_HEAD_DIM ** -0.5)
    p = jax.nn.softmax(scores, axis=-1).astype(v.dtype)
    o = jnp.einsum('bhsm,bmhd->bshd', p, v).reshape(bsz, s, X_WIDTH)
    return o @ w_xo


def swiglu(xn, w_gate, w_up, w_down):
    return (jax.nn.silu(xn @ w_gate) * (xn @ w_up)) @ w_down


def setup_inputs(seed: int = 0) -> dict:
    key = jax.random.key(seed)
    ks = jax.random.split(key, 24)
    f32 = jnp.float32

    def w(k, shape, fan_in):
        return jax.random.normal(k, shape, f32) * (fan_in ** -0.5)

    def gain(k, shape):
        return 1.0 + 0.05 * jax.random.normal(k, shape, f32)

    return {
        'x': jax.random.normal(ks[0], (BATCH, SEQ, D_MODEL), f32),
        'mem': jax.random.normal(ks[1], (BATCH, MEM_LEN, D_MODEL), f32),
        'w_in': w(ks[2], (DEPTH, D_MODEL, D_IN), D_MODEL),
        'w_conv': w(ks[3], (DEPTH, CONV_KSIZE, CONV_WIDTH), CONV_KSIZE),
        'rel_bias': 0.1 * jax.random.normal(ks[4], (DEPTH, ATT_HEADS, 2 * MAX_REL + 1), f32),
        'w_pool': w(ks[5], (DEPTH, POOL_GROUPS, POOL_GROUP_WIDTH, POOL_GROUP_WIDTH), POOL_GROUP_WIDTH),
        'pool_scale': 1.0 + 0.1 * jax.random.normal(ks[6], (DEPTH, POOL_WIDTH), f32),
        'w_out': w(ks[7], (DEPTH, D_MIX, D_MODEL), D_MIX),
        'w_xq': w(ks[8], (DEPTH, D_MODEL, X_WIDTH), D_MODEL),
        'w_xkv': w(ks[9], (DEPTH, D_MODEL, 2 * X_WIDTH), D_MODEL),
        'w_xo': w(ks[10], (DEPTH, X_WIDTH, D_MODEL), X_WIDTH),
        'w_gate': w(ks[11], (DEPTH, D_MODEL, D_FF), D_MODEL),
        'w_up': w(ks[12], (DEPTH, D_MODEL, D_FF), D_MODEL),
        'w_down': w(ks[13], (DEPTH, D_FF, D_MODEL), D_FF),
        'g_mix_pre': gain(ks[14], (DEPTH, D_MODEL)),
        'g_mix_post': gain(ks[15], (DEPTH, D_MODEL)),
        'g_x_pre': gain(ks[16], (DEPTH, D_MODEL)),
        'g_x_post': gain(ks[17], (DEPTH, D_MODEL)),
        'g_ffn_pre': gain(ks[18], (DEPTH, D_MODEL)),
        'g_ffn_post': gain(ks[19], (DEPTH, D_MODEL)),
        'g_mem': gain(ks[20], (D_MODEL,)),
    }


def reference(x, mem, w_in, w_conv, rel_bias, w_pool, pool_scale, w_out,
              w_xq, w_xkv, w_xo, w_gate, w_up, w_down,
              g_mix_pre, g_mix_post, g_x_pre, g_x_post, g_ffn_pre, g_ffn_post, g_mem):
    bsz, s, _ = x.shape
    memn = rmsnorm(mem, g_mem)
    for l in range(DEPTH):
        xn = rmsnorm(x, g_mix_pre[l])
        proj = xn @ w_in[l]
        b, c, h, q, k, v, u = jnp.split(proj, SPLITS, axis=-1)
        y_conv = short_conv_mixer(b, c, h, w_conv[l])
        y_att = chunk_attention(q.reshape(bsz, s, ATT_HEADS, ATT_HEAD_DIM),
                                k.reshape(bsz, s, ATT_HEADS, ATT_HEAD_DIM),
                                v.reshape(bsz, s, ATT_HEADS, ATT_HEAD_DIM), rel_bias[l])
        y_pool = pool_mixer(u, w_pool[l], pool_scale[l])
        mix = jnp.concatenate([y_conv, y_att, y_pool], axis=-1) @ w_out[l]
        x = x + rmsnorm(mix, g_mix_post[l])
        xc = memory_cross_attention(rmsnorm(x, g_x_pre[l]), memn, w_xq[l], w_xkv[l], w_xo[l])
        x = x + rmsnorm(xc, g_x_post[l])
        f = swiglu(rmsnorm(x, g_ffn_pre[l]), w_gate[l], w_up[l], w_down[l])
        x = x + rmsnorm(f, g_ffn_post[l])
    return x
```

```python
import functools

import jax
import jax.numpy as jnp
import numpy as np
from jax import lax
from jax.experimental import pallas as pl
from jax.experimental.pallas import tpu as pltpu

CHUNK = 64
LEFT_CHUNKS = 8
MAX_REL = 256
POOL_WINDOWS = (2, 4, 8, 16)
CONV_KSIZE = 3
ATT_HEAD_DIM = 128
X_HEADS = 4
RMS_EPS = 1e-6
NEG_INF = -1e30

V7X_VMEM_BYTES = 64 * 1024 * 1024
V7X_LANES = 128
V7X_BF16_SUBLANES = 16
V7X_MXU_DIM = 256

ATT_TQ = LEFT_CHUNKS * CHUNK
HALO = V7X_BF16_SUBLANES
EPI_ROWS = 16
EPI_COLS = 1024

BF16 = jnp.bfloat16
F32 = jnp.float32


def _vmem_limit(*buffer_bytes):
    need = int(sum(buffer_bytes)) + 8 * 1024 * 1024
    return min(need, V7X_VMEM_BYTES - 4 * 1024 * 1024)


def _params(semantics, *buffer_bytes):
    return pltpu.CompilerParams(dimension_semantics=semantics, vmem_limit_bytes=_vmem_limit(*buffer_bytes))


def _tile(n, pref):
    if n <= pref:
        return n
    t = pref
    while n % t:
        t -= V7X_LANES
    assert t > 0
    return t


def _rmsnorm_kernel(x_ref, g_ref, o_ref):
    x = x_ref[...]
    inv = lax.rsqrt(jnp.mean(x * x, axis=-1, keepdims=True) + RMS_EPS)
    o_ref[...] = (x * inv * g_ref[...]).astype(o_ref.dtype)


def rmsnorm_bf16(x, g):
    rows, d = x.shape
    br = _tile(rows, 256)
    return pl.pallas_call(
        _rmsnorm_kernel,
        grid=(rows // br,),
        in_specs=[pl.BlockSpec((br, d), lambda i: (i, 0)), pl.BlockSpec((1, d), lambda i: (0, 0))],
        out_specs=pl.BlockSpec((br, d), lambda i: (i, 0)),
        out_shape=jax.ShapeDtypeStruct((rows, d), BF16),
        compiler_params=_params(("parallel",), 2 * br * d * 4, 2 * br * d * 2),
        name="rmsnorm",
    )(x, g.reshape(1, d))


def _mm_kernel(a_ref, w_ref, o_ref):
    o_ref[...] = jnp.dot(a_ref[...], w_ref[...], preferred_element_type=F32).astype(o_ref.dtype)


def matmul(a, w, *, bm=1024, bn=1024, name="matmul"):
    m, k = a.shape
    n = w.shape[1]
    bm, bn = _tile(m, bm), _tile(n, bn)
    return pl.pallas_call(
        _mm_kernel,
        grid=(m // bm, n // bn),
        in_specs=[pl.BlockSpec((bm, k), lambda i, j: (i, 0)), pl.BlockSpec((k, bn), lambda i, j: (0, j))],
        out_specs=pl.BlockSpec((bm, bn), lambda i, j: (i, j)),
        out_shape=jax.ShapeDtypeStruct((m, n), BF16),
        compiler_params=_params(("parallel", "arbitrary"), 2 * bm * k * 2, 2 * k * bn * 2, 2 * bm * bn * 2,
                                bm * bn * 4),
        name=name,
    )(a, w)


def _gateup_kernel(a_ref, wg_ref, wu_ref, o_ref):
    a = a_ref[...]
    g = jnp.dot(a, wg_ref[...], preferred_element_type=F32)
    u = jnp.dot(a, wu_ref[...], preferred_element_type=F32)
    o_ref[...] = (g * jax.nn.sigmoid(g) * u).astype(o_ref.dtype)


def gate_up(a, wg, wu, *, bm=1024, bn=512):
    m, k = a.shape
    n = wg.shape[1]
    bm, bn = _tile(m, bm), _tile(n, bn)
    return pl.pallas_call(
        _gateup_kernel,
        grid=(m // bm, n // bn),
        in_specs=[pl.BlockSpec((bm, k), lambda i, j: (i, 0)),
                  pl.BlockSpec((k, bn), lambda i, j: (0, j)),
                  pl.BlockSpec((k, bn), lambda i, j: (0, j))],
        out_specs=pl.BlockSpec((bm, bn), lambda i, j: (i, j)),
        out_shape=jax.ShapeDtypeStruct((m, n), BF16),
        compiler_params=_params(("parallel", "arbitrary"), 2 * bm * k * 2, 4 * k * bn * 2, 2 * bm * bn * 2,
                                3 * bm * bn * 4),
        name="gate_up",
    )(a, wg, wu)


def _mm_epilogue_kernel(a_ref, w_ref, x_ref, gpost_ref, gnext_ref, xo_ref, *maybe_xn_ref, nk):
    k = pl.program_id(1)
    a = a_ref[...]
    for c0 in range(0, xo_ref.shape[1], EPI_COLS):
        cols = slice(c0, c0 + EPI_COLS)
        part = jnp.dot(a, w_ref[:, cols], preferred_element_type=F32)

        @pl.when(k == 0)
        def _():
            xo_ref[:, cols] = part

        @pl.when(k > 0)
        def _():
            xo_ref[:, cols] += part

    @pl.when(k == nk - 1)
    def _():
        gpost = gpost_ref[...]
        gnext = gnext_ref[...]

        def body(r, carry):
            rows = pl.ds(pl.multiple_of(r * EPI_ROWS, EPI_ROWS), EPI_ROWS)
            m = xo_ref[rows, :]
            y = m * lax.rsqrt(jnp.mean(m * m, axis=-1, keepdims=True) + RMS_EPS) * gpost
            xnew = x_ref[rows, :] + y
            xo_ref[rows, :] = xnew
            if maybe_xn_ref:
                inv = lax.rsqrt(jnp.mean(xnew * xnew, axis=-1, keepdims=True) + RMS_EPS)
                maybe_xn_ref[0][rows, :] = (xnew * inv * gnext).astype(BF16)
            return carry

        lax.fori_loop(0, xo_ref.shape[0] // EPI_ROWS, body, 0)


def matmul_epilogue(a, w, x, gpost, gnext, *, emit_xn=True, bm=512, bk=512):
    m, kdim = a.shape
    d = w.shape[1]
    bm, bk = _tile(m, bm), _tile(kdim, bk)
    nk = kdim // bk
    row_spec = pl.BlockSpec((bm, d), lambda i, k: (i, 0))
    vec_spec = pl.BlockSpec((1, d), lambda i, k: (0, 0))
    out_shape = [jax.ShapeDtypeStruct((m, d), F32)]
    out_specs = [row_spec]
    if emit_xn:
        out_shape.append(jax.ShapeDtypeStruct((m, d), BF16))
        out_specs.append(row_spec)
    res = pl.pallas_call(
        functools.partial(_mm_epilogue_kernel, nk=nk),
        grid=(m // bm, nk),
        in_specs=[pl.BlockSpec((bm, bk), lambda i, k: (i, k)),
                  pl.BlockSpec((bk, d), lambda i, k: (k, 0)),
                  row_spec, vec_spec, vec_spec],
        out_specs=out_specs,
        out_shape=out_shape,
        compiler_params=_params(("parallel", "arbitrary"), 2 * bm * bk * 2, 2 * bk * d * 2, 4 * bm * d * 4,
                                2 * bm * d * 2 * int(emit_xn)),
        name="matmul_epilogue",
    )(a, w, x, gpost.reshape(1, d), gnext.reshape(1, d))
    return (res[0], res[1]) if emit_xn else (res[0], None)


def _conv_kernel(b_ref, c_ref, h_ref, chalo_ref, hhalo_ref, w_ref, o_ref):
    i = pl.program_id(0)
    u = c_ref[...].astype(F32) * h_ref[...].astype(F32)
    uh = chalo_ref[...].astype(F32) * hhalo_ref[...].astype(F32)
    uh = jnp.where(i > 0, uh, 0.0)
    ext = jnp.concatenate([uh, u], axis=0)
    w = w_ref[...]
    conv = pltpu.roll(ext, CONV_KSIZE - 1, 0) * w[0:1]
    for tap in range(1, CONV_KSIZE):
        shift = CONV_KSIZE - 1 - tap
        shifted = pltpu.roll(ext, shift, 0) if shift else ext
        conv = conv + shifted * w[tap:tap + 1]
    o_ref[...] = (b_ref[...].astype(F32) * conv[HALO:]).astype(o_ref.dtype)


def conv_mixer(proj, w_conv, width, *, bm=512):
    s = proj.shape[0]
    bm = _tile(s, bm)
    hb = bm // HALO
    blk = lambda col: pl.BlockSpec((bm, width), lambda i: (i, col))
    halo = lambda col: pl.BlockSpec((HALO, width), lambda i: (jnp.maximum(i * hb - 1, 0), col))
    return pl.pallas_call(
        _conv_kernel,
        grid=(s // bm,),
        in_specs=[blk(0), blk(1), blk(2), halo(1), halo(2), pl.BlockSpec((CONV_KSIZE, width), lambda i: (0, 0))],
        out_specs=pl.BlockSpec((bm, width), lambda i: (i, 0)),
        out_shape=jax.ShapeDtypeStruct((s, width), BF16),
        compiler_params=_params(("parallel",), 8 * bm * width * 2, 6 * bm * width * 4),
        name="conv_mixer",
    )(proj, proj, proj, proj, proj, w_conv)


def _pool_kernel(u_ref, uhalo_ref, w_ref, scale_ref, o_ref, *, gw):
    i = pl.program_id(0)
    bm = u_ref.shape[0]
    t = i * bm + lax.broadcasted_iota(jnp.int32, (bm, 1), 0)
    for g, win in enumerate(POOL_WINDOWS):
        cols = slice(g * gw, (g + 1) * gw)
        u = u_ref[:, cols].astype(F32)
        uh = jnp.where(i > 0, uhalo_ref[:, cols].astype(F32), 0.0)
        acc = jnp.concatenate([uh, u], axis=0)
        span = 1
        while span < win:
            acc = acc + pltpu.roll(acc, span, 0)
            span *= 2
        count = jnp.minimum(t + 1, win).astype(F32)
        pooled = (acc[HALO:] / count - u).astype(BF16)
        mixed = jnp.dot(pooled, w_ref[g], preferred_element_type=F32)
        o_ref[:, cols] = (mixed * scale_ref[:, cols]).astype(o_ref.dtype)


def pool_mixer(proj, w_pool, pool_scale, col_block, *, bm=512):
    s = proj.shape[0]
    groups, gw, _ = w_pool.shape
    width = groups * gw
    bm = _tile(s, bm)
    hb = bm // HALO
    return pl.pallas_call(
        functools.partial(_pool_kernel, gw=gw),
        grid=(s // bm,),
        in_specs=[pl.BlockSpec((bm, width), lambda i: (i, col_block)),
                  pl.BlockSpec((HALO, width), lambda i: (jnp.maximum(i * hb - 1, 0), col_block)),
                  pl.BlockSpec((groups, gw, gw), lambda i: (0, 0, 0)),
                  pl.BlockSpec((1, width), lambda i: (0, 0))],
        out_specs=pl.BlockSpec((bm, width), lambda i: (i, 0)),
        out_shape=jax.ShapeDtypeStruct((s, width), BF16),
        compiler_params=_params(("parallel",), 4 * bm * width * 2, 6 * bm * width * 4),
        name="pool_mixer",
    )(proj, proj, w_pool, pool_scale.reshape(1, width))


def _attn_kernel(q_ref, kp_ref, kc_ref, vp_ref, vc_ref, bias_ref, o_ref, *, scale):
    qi = pl.program_id(1)
    tq = q_ref.shape[0]
    k = jnp.concatenate([kp_ref[...], kc_ref[...]], axis=0)
    v = jnp.concatenate([vp_ref[...], vc_ref[...]], axis=0)
    s = lax.dot_general(q_ref[...], k, (((1,), (1,)), ((), ())), preferred_element_type=F32) * scale
    s = s + bias_ref[0]
    col = lax.broadcasted_iota(jnp.int32, s.shape, 1)
    s = jnp.where((qi > 0) | (col >= tq), s, NEG_INF)
    m = jnp.max(s, axis=-1, keepdims=True)
    p = jnp.exp(s - m)
    l = jnp.sum(p, axis=-1, keepdims=True)
    o = jnp.dot(p.astype(BF16), v, preferred_element_type=F32) / l
    o_ref[...] = o.astype(o_ref.dtype)


def _band_bias(rel_bias):
    r = np.arange(ATT_TQ)[:, None]
    c = np.arange(2 * ATT_TQ)[None, :]
    idx = np.clip(r - c + ATT_TQ, -MAX_REL, MAX_REL) + MAX_REL
    qc, kc = r // CHUNK, c // CHUNK
    band = (kc >= qc) & (kc <= qc + LEFT_CHUNKS)
    return jnp.where(jnp.asarray(band)[None], rel_bias[:, idx].astype(F32), NEG_INF)


def chunk_attention(proj, rel_bias, q_col, k_col, v_col, heads):
    s = proj.shape[0]
    assert s % ATT_TQ == 0
    dh = ATT_HEAD_DIM
    bias = _band_bias(rel_bias)
    cur = lambda col: pl.BlockSpec((ATT_TQ, dh), lambda h, i: (i, col + h))
    prev = lambda col: pl.BlockSpec((ATT_TQ, dh), lambda h, i: (jnp.maximum(i - 1, 0), col + h))
    return pl.pallas_call(
        functools.partial(_attn_kernel, scale=dh ** -0.5),
        grid=(heads, s // ATT_TQ),
        in_specs=[cur(q_col), prev(k_col), cur(k_col), prev(v_col), cur(v_col),
                  pl.BlockSpec((1, ATT_TQ, 2 * ATT_TQ), lambda h, i: (h, 0, 0))],
        out_specs=pl.BlockSpec((ATT_TQ, dh), lambda h, i: (i, h)),
        out_shape=jax.ShapeDtypeStruct((s, heads * dh), BF16),
        compiler_params=_params(("parallel", "arbitrary"), 12 * ATT_TQ * dh * 2, 2 * ATT_TQ * 2 * ATT_TQ * 4,
                                4 * ATT_TQ * 2 * ATT_TQ * 4),
        name="chunk_attention",
    )(proj, proj, proj, proj, proj, bias)


def _xattn_kernel(q_ref, k_ref, v_ref, o_ref, *, scale):
    s = lax.dot_general(q_ref[...], k_ref[...], (((1,), (1,)), ((), ())), preferred_element_type=F32) * scale
    m = jnp.max(s, axis=-1, keepdims=True)
    p = jnp.exp(s - m)
    l = jnp.sum(p, axis=-1, keepdims=True)
    o = jnp.dot(p.astype(BF16), v_ref[...], preferred_element_type=F32) / l
    o_ref[...] = o.astype(o_ref.dtype)


def cross_attention(q, kv, *, bm=1024):
    s, xw = q.shape
    mem_len = kv.shape[0]
    dh = xw // X_HEADS
    bm = _tile(s, bm)
    return pl.pallas_call(
        functools.partial(_xattn_kernel, scale=dh ** -0.5),
        grid=(s // bm, X_HEADS),
        in_specs=[pl.BlockSpec((bm, dh), lambda i, h: (i, h)),
                  pl.BlockSpec((mem_len, dh), lambda i, h: (0, h)),
                  pl.BlockSpec((mem_len, dh), lambda i, h: (0, X_HEADS + h))],
        out_specs=pl.BlockSpec((bm, dh), lambda i, h: (i, h)),
        out_shape=jax.ShapeDtypeStruct((s, xw), BF16),
        compiler_params=_params(("parallel", "arbitrary"), 4 * bm * dh * 2, 4 * mem_len * dh * 2,
                                4 * bm * mem_len * 4),
        name="cross_attention",
    )(q, kv, kv)


def kernel(x, mem, w_in, w_conv, rel_bias, w_pool, pool_scale, w_out, w_xq, w_xkv, w_xo, w_gate, w_up, w_down,
           g_mix_pre, g_mix_post, g_x_pre, g_x_post, g_ffn_pre, g_ffn_post, g_mem):
    bsz, s, d = x.shape
    assert bsz == 1
    depth = w_in.shape[0]
    conv_w = w_conv.shape[2]
    pool_w = pool_scale.shape[1]
    att_w = w_out.shape[1] - conv_w - pool_w
    heads = att_w // ATT_HEAD_DIM
    d_ff = w_gate.shape[2]
    ff_pad = -d_ff % 1024

    xs = x[0]
    memn = rmsnorm_bf16(mem[0], g_mem)
    xn = rmsnorm_bf16(xs, g_mix_pre[0])
    for l in range(depth):
        proj = matmul(xn, w_in[l].astype(BF16), name="in_proj")
        y_conv = conv_mixer(proj, w_conv[l], conv_w)
        q_col = 3 * conv_w // ATT_HEAD_DIM
        y_att = chunk_attention(proj, rel_bias[l], q_col, q_col + heads, q_col + 2 * heads, heads)
        y_pool = pool_mixer(proj, w_pool[l].astype(BF16), pool_scale[l], (3 * conv_w + 3 * att_w) // pool_w)
        mixed = jnp.concatenate([y_conv, y_att, y_pool], axis=-1)
        xs, xn = matmul_epilogue(mixed, w_out[l].astype(BF16), xs, g_mix_post[l], g_x_pre[l])
        q = matmul(xn, w_xq[l].astype(BF16), name="xq_proj")
        kv = matmul(memn, w_xkv[l].astype(BF16), name="xkv_proj")
        o = cross_attention(q, kv)
        xs, xn = matmul_epilogue(o, w_xo[l].astype(BF16), xs, g_x_post[l], g_ffn_pre[l])
        wg = jnp.pad(w_gate[l].astype(BF16), ((0, 0), (0, ff_pad)))
        wu = jnp.pad(w_up[l].astype(BF16), ((0, 0), (0, ff_pad)))
        wd = jnp.pad(w_down[l].astype(BF16), ((0, ff_pad), (0, 0)))
        hid = gate_up(xn, wg, wu)
        last = l == depth - 1
        g_next = g_ffn_post[l] if last else g_mix_pre[l + 1]
        xs, xn = matmul_epilogue(hid, wd, xs, g_ffn_post[l], g_next, emit_xn=not last)
    return xs[None]
```

```python
import functools

import jax
import jax.numpy as jnp
import numpy as np
from jax import lax
from jax.experimental import pallas as pl
from jax.experimental.pallas import tpu as pltpu

CHUNK = 64
LEFT_CHUNKS = 8
MAX_REL = 256
POOL_WINDOWS = (2, 4, 8, 16)
CONV_KSIZE = 3
ATT_HEAD_DIM = 128
X_HEADS = 4
RMS_EPS = 1e-6
NEG_INF = -1e30

V7X_VMEM_BYTES = 64 * 1024 * 1024
V7X_LANES = 128
V7X_BF16_SUBLANES = 16

ATT_TQ = LEFT_CHUNKS * CHUNK
HALO = V7X_BF16_SUBLANES
EPI_ROWS = V7X_BF16_SUBLANES
EPI_COLS = 1024
FF_ALIGN = 1024

BF16 = jnp.bfloat16
F32 = jnp.float32


def _vmem_limit(*buffer_bytes):
    need = int(sum(buffer_bytes)) + 8 * 1024 * 1024
    return min(need, V7X_VMEM_BYTES - 4 * 1024 * 1024)


def _params(semantics, *buffer_bytes):
    return pltpu.CompilerParams(dimension_semantics=semantics, vmem_limit_bytes=_vmem_limit(*buffer_bytes))


def _tile(n, pref):
    if n <= pref:
        return n
    t = pref
    while n % t:
        t -= V7X_LANES
    assert t > 0
    return t


def _rmsnorm_kernel(x_ref, g_ref, o_ref):
    x = x_ref[...]
    inv = lax.rsqrt(jnp.mean(x * x, axis=-1, keepdims=True) + RMS_EPS)
    o_ref[...] = (x * inv * g_ref[...]).astype(o_ref.dtype)


def rmsnorm_bf16(x, g):
    rows, d = x.shape
    br = _tile(rows, 256)
    return pl.pallas_call(
        _rmsnorm_kernel,
        grid=(rows // br,),
        in_specs=[pl.BlockSpec((br, d), lambda i: (i, 0)), pl.BlockSpec((1, d), lambda i: (0, 0))],
        out_specs=pl.BlockSpec((br, d), lambda i: (i, 0)),
        out_shape=jax.ShapeDtypeStruct((rows, d), BF16),
        compiler_params=_params(("parallel",), 2 * br * d * 4, 2 * br * d * 2),
        name="rmsnorm",
    )(x, g.reshape(1, d))


def _mm_kernel(a_ref, w_ref, o_ref):
    o_ref[...] = jnp.dot(a_ref[...], w_ref[...], preferred_element_type=F32).astype(o_ref.dtype)


def matmul(a, w, *, bm=1024, bn=1024, name="matmul"):
    m, k = a.shape
    n = w.shape[1]
    bm, bn = _tile(m, bm), _tile(n, bn)
    return pl.pallas_call(
        _mm_kernel,
        grid=(m // bm, n // bn),
        in_specs=[pl.BlockSpec((bm, k), lambda i, j: (i, 0)), pl.BlockSpec((k, bn), lambda i, j: (0, j))],
        out_specs=pl.BlockSpec((bm, bn), lambda i, j: (i, j)),
        out_shape=jax.ShapeDtypeStruct((m, n), BF16),
        compiler_params=_params(("parallel", "arbitrary"), 2 * bm * k * 2, 2 * k * bn * 2, 2 * bm * bn * 2,
                                bm * bn * 4),
        name=name,
    )(a, w)


def _gateup_kernel(a_ref, wg_ref, wu_ref, o_ref):
    a = a_ref[...]
    g = jnp.dot(a, wg_ref[...], preferred_element_type=F32)
    u = jnp.dot(a, wu_ref[...], preferred_element_type=F32)
    o_ref[...] = (g * jax.nn.sigmoid(g) * u).astype(o_ref.dtype)


def gate_up(a, wg, wu, *, bm=1024, bn=512):
    m, k = a.shape
    n = wg.shape[1]
    bm, bn = _tile(m, bm), _tile(n, bn)
    return pl.pallas_call(
        _gateup_kernel,
        grid=(m // bm, n // bn),
        in_specs=[pl.BlockSpec((bm, k), lambda i, j: (i, 0)),
                  pl.BlockSpec((k, bn), lambda i, j: (0, j)),
                  pl.BlockSpec((k, bn), lambda i, j: (0, j))],
        out_specs=pl.BlockSpec((bm, bn), lambda i, j: (i, j)),
        out_shape=jax.ShapeDtypeStruct((m, n), BF16),
        compiler_params=_params(("parallel", "arbitrary"), 2 * bm * k * 2, 4 * k * bn * 2, 2 * bm * bn * 2,
                                3 * bm * bn * 4),
        name="gate_up",
    )(a, wg, wu)


def _mm_epilogue_kernel(*refs, seg_steps, nb, nk, nc, rc, emit_xn):
    nseg = len(seg_steps)
    a_refs = refs[:nseg]
    w_ref, x_ref, gpost_ref, gnext_ref, xo_ref = refs[nseg:nseg + 5]
    xn_ref = refs[nseg + 5] if emit_xn else None
    acc_ref = refs[-1]
    i = pl.program_id(0)
    k = pl.program_id(1)
    slot = lax.rem(i, 2)
    d = xo_ref.shape[1]

    @pl.when((i < nb) & (k == 0))
    def _():
        acc_ref[slot] = jnp.zeros(acc_ref.shape[1:], F32)

    @pl.when(i < nb)
    def _():
        k0 = 0
        for a_ref, steps in zip(a_refs, seg_steps):
            def accumulate(a_ref=a_ref):
                a = a_ref[...]
                for c0 in range(0, d, EPI_COLS):
                    cols = slice(c0, c0 + EPI_COLS)
                    acc_ref[slot, :, cols] += jnp.dot(a, w_ref[:, cols], preferred_element_type=F32)

            if nseg == 1:
                accumulate()
            else:
                pl.when((k >= k0) & (k < k0 + steps))(accumulate)
            k0 += steps

    @pl.when((i > 0) & (k < nc))
    def _():
        gpost = gpost_ref[...]
        gnext = gnext_ref[...]
        for r0 in range(0, rc, EPI_ROWS):
            rows = pl.ds(pl.multiple_of(k * rc + r0, EPI_ROWS), EPI_ROWS)
            out_rows = slice(r0, r0 + EPI_ROWS)
            m = acc_ref[1 - slot, rows, :]
            y = m * lax.rsqrt(jnp.mean(m * m, axis=-1, keepdims=True) + RMS_EPS) * gpost
            xnew = x_ref[out_rows, :] + y
            xo_ref[out_rows, :] = xnew
            if emit_xn:
                inv = lax.rsqrt(jnp.mean(xnew * xnew, axis=-1, keepdims=True) + RMS_EPS)
                xn_ref[out_rows, :] = (xnew * inv * gnext).astype(BF16)


def matmul_epilogue(a_segs, w, x, gpost, gnext, *, emit_xn=True, bm=1024, bk=512, rc=64):
    m, d = x.shape
    bm = _tile(m, bm)
    kdim = w.shape[0]
    bk = min([bk] + [a.shape[1] for a in a_segs])
    seg_steps = tuple(a.shape[1] // bk for a in a_segs)
    assert all(a.shape[1] % bk == 0 for a in a_segs) and sum(a.shape[1] for a in a_segs) == kdim
    assert d % EPI_COLS == 0 or d < EPI_COLS
    nk = kdim // bk
    nb = m // bm
    while bm % rc or bm // rc > nk:
        rc += EPI_ROWS
    nc = bm // rc

    def seg_spec(k0, steps):
        def imap(i, k):
            kk = jnp.where(i < nb, jnp.clip(k - k0, 0, steps - 1), steps - 1)
            return (jnp.minimum(i, nb - 1), kk)
        return pl.BlockSpec((bm, bk), imap)

    seg_specs, k0 = [], 0
    for steps in seg_steps:
        seg_specs.append(seg_spec(k0, steps))
        k0 += steps

    def chunk_map(i, k):
        return (jnp.where(i == 0, 0, (i - 1) * nc + jnp.minimum(k, nc - 1)), 0)

    chunk_spec = pl.BlockSpec((rc, d), chunk_map)
    vec_spec = pl.BlockSpec((1, d), lambda i, k: (0, 0))
    out_shape = [jax.ShapeDtypeStruct((m, d), F32)]
    out_specs = [chunk_spec]
    if emit_xn:
        out_shape.append(jax.ShapeDtypeStruct((m, d), BF16))
        out_specs.append(chunk_spec)
    res = pl.pallas_call(
        functools.partial(_mm_epilogue_kernel, seg_steps=seg_steps, nb=nb, nk=nk, nc=nc, rc=rc, emit_xn=emit_xn),
        grid=(nb + 1, nk),
        in_specs=seg_specs + [pl.BlockSpec((bk, d), lambda i, k: (jnp.where(i < nb, k, nk - 1), 0)),
                              chunk_spec, vec_spec, vec_spec],
        out_specs=out_specs,
        out_shape=out_shape,
        scratch_shapes=[pltpu.VMEM((2, bm, d), F32)],
        compiler_params=_params(("arbitrary", "arbitrary"), 2 * bm * d * 4, 2 * len(a_segs) * bm * bk * 2,
                                2 * bk * d * 2, 4 * rc * d * 4, 2 * rc * d * 2, bm * min(d, EPI_COLS) * 4),
        name="matmul_epilogue",
    )(*a_segs, w, x, gpost.reshape(1, d), gnext.reshape(1, d))
    return (res[0], res[1]) if emit_xn else (res[0], None)


def _conv_kernel(b_ref, c_ref, h_ref, chalo_ref, hhalo_ref, w_ref, o_ref):
    i = pl.program_id(0)
    u = c_ref[...].astype(F32) * h_ref[...].astype(F32)
    uh = chalo_ref[...].astype(F32) * hhalo_ref[...].astype(F32)
    uh = jnp.where(i > 0, uh, 0.0)
    ext = jnp.concatenate([uh, u], axis=0)
    w = w_ref[...]
    conv = pltpu.roll(ext, CONV_KSIZE - 1, 0) * w[0:1]
    for tap in range(1, CONV_KSIZE):
        shift = CONV_KSIZE - 1 - tap
        shifted = pltpu.roll(ext, shift, 0) if shift else ext
        conv = conv + shifted * w[tap:tap + 1]
    o_ref[...] = (b_ref[...].astype(F32) * conv[HALO:]).astype(o_ref.dtype)


def conv_mixer(proj, w_conv, width, *, bm=512):
    s = proj.shape[0]
    bm = _tile(s, bm)
    hb = bm // HALO
    blk = lambda col: pl.BlockSpec((bm, width), lambda i: (i, col))
    halo = lambda col: pl.BlockSpec((HALO, width), lambda i: (jnp.maximum(i * hb - 1, 0), col))
    return pl.pallas_call(
        _conv_kernel,
        grid=(s // bm,),
        in_specs=[blk(0), blk(1), blk(2), halo(1), halo(2), pl.BlockSpec((CONV_KSIZE, width), lambda i: (0, 0))],
        out_specs=pl.BlockSpec((bm, width), lambda i: (i, 0)),
        out_shape=jax.ShapeDtypeStruct((s, width), BF16),
        compiler_params=_params(("parallel",), 8 * bm * width * 2, 6 * bm * width * 4),
        name="conv_mixer",
    )(proj, proj, proj, proj, proj, w_conv)


def _pool_kernel(u_ref, uhalo_ref, w_ref, scale_ref, o_ref, *, gw):
    i = pl.program_id(0)
    bm = u_ref.shape[0]
    t = i * bm + lax.broadcasted_iota(jnp.int32, (bm, 1), 0)
    for g, win in enumerate(POOL_WINDOWS):
        cols = slice(g * gw, (g + 1) * gw)
        u = u_ref[:, cols].astype(F32)
        uh = jnp.where(i > 0, uhalo_ref[:, cols].astype(F32), 0.0)
        acc = jnp.concatenate([uh, u], axis=0)
        span = 1
        while span < win:
            acc = acc + pltpu.roll(acc, span, 0)
            span *= 2
        count = jnp.minimum(t + 1, win).astype(F32)
        pooled = (acc[HALO:] / count - u).astype(BF16)
        mixed = jnp.dot(pooled, w_ref[g], preferred_element_type=F32)
        o_ref[:, cols] = (mixed * scale_ref[:, cols]).astype(o_ref.dtype)


def pool_mixer(proj, w_pool, pool_scale, col_block, *, bm=512):
    s = proj.shape[0]
    groups, gw, _ = w_pool.shape
    width = groups * gw
    bm = _tile(s, bm)
    hb = bm // HALO
    return pl.pallas_call(
        functools.partial(_pool_kernel, gw=gw),
        grid=(s // bm,),
        in_specs=[pl.BlockSpec((bm, width), lambda i: (i, col_block)),
                  pl.BlockSpec((HALO, width), lambda i: (jnp.maximum(i * hb - 1, 0), col_block)),
                  pl.BlockSpec((groups, gw, gw), lambda i: (0, 0, 0)),
                  pl.BlockSpec((1, width), lambda i: (0, 0))],
        out_specs=pl.BlockSpec((bm, width), lambda i: (i, 0)),
        out_shape=jax.ShapeDtypeStruct((s, width), BF16),
        compiler_params=_params(("parallel",), 4 * bm * width * 2, 6 * bm * width * 4),
        name="pool_mixer",
    )(proj, proj, w_pool, pool_scale.reshape(1, width))


def _attn_kernel(q_ref, kp_ref, kc_ref, vp_ref, vc_ref, bvec_ref, o_ref, bias_ref, *, scale):
    qi = pl.program_id(1)
    tq = q_ref.shape[0]

    @pl.when(qi == 0)
    def _():
        shape = (tq, 2 * tq)
        toeplitz = pltpu.roll(jnp.broadcast_to(bvec_ref[0], shape), 0, 1, stride=1, stride_axis=0)
        qc = lax.broadcasted_iota(jnp.int32, shape, 0) // CHUNK
        kc = lax.broadcasted_iota(jnp.int32, shape, 1) // CHUNK
        bias_ref[...] = jnp.where((kc >= qc) & (kc <= qc + LEFT_CHUNKS), toeplitz, NEG_INF)

    k = jnp.concatenate([kp_ref[...], kc_ref[...]], axis=0)
    v = jnp.concatenate([vp_ref[...], vc_ref[...]], axis=0)
    s = lax.dot_general(q_ref[...], k, (((1,), (1,)), ((), ())), preferred_element_type=F32) * scale
    s = s + bias_ref[...]
    col = lax.broadcasted_iota(jnp.int32, s.shape, 1)
    s = jnp.where((qi > 0) | (col >= tq), s, NEG_INF)
    m = jnp.max(s, axis=-1, keepdims=True)
    p = jnp.exp(s - m)
    l = jnp.sum(p, axis=-1, keepdims=True)
    o = jnp.dot(p.astype(BF16), v, preferred_element_type=F32) / l
    o_ref[...] = o.astype(o_ref.dtype)


def _bias_by_offset(rel_bias):
    far = jnp.broadcast_to(rel_bias[:, 2 * MAX_REL:], (rel_bias.shape[0], ATT_TQ - MAX_REL))
    near = rel_bias[:, :0:-1]
    return jnp.concatenate([far, near, far], axis=1)[:, None, :].astype(F32)


def chunk_attention(proj, rel_bias, q_col, k_col, v_col, heads):
    s = proj.shape[0]
    assert s % ATT_TQ == 0 and ATT_TQ == 2 * MAX_REL
    dh = ATT_HEAD_DIM
    cur = lambda col: pl.BlockSpec((ATT_TQ, dh), lambda h, i: (i, col + h))
    prev = lambda col: pl.BlockSpec((ATT_TQ, dh), lambda h, i: (jnp.maximum(i - 1, 0), col + h))
    return pl.pallas_call(
        functools.partial(_attn_kernel, scale=dh ** -0.5),
        grid=(heads, s // ATT_TQ),
        in_specs=[cur(q_col), prev(k_col), cur(k_col), prev(v_col), cur(v_col),
                  pl.BlockSpec((1, 1, 2 * ATT_TQ), lambda h, i: (h, 0, 0))],
        out_specs=pl.BlockSpec((ATT_TQ, dh), lambda h, i: (i, h)),
        out_shape=jax.ShapeDtypeStruct((s, heads * dh), BF16),
        scratch_shapes=[pltpu.VMEM((ATT_TQ, 2 * ATT_TQ), F32)],
        compiler_params=_params(("parallel", "arbitrary"), 12 * ATT_TQ * dh * 2, ATT_TQ * 2 * ATT_TQ * 4,
                                4 * ATT_TQ * 2 * ATT_TQ * 4),
        name="chunk_attention",
    )(proj, proj, proj, proj, proj, _bias_by_offset(rel_bias))


def _xattn_kernel(q_ref, k_ref, v_ref, o_ref, *, scale):
    s = lax.dot_general(q_ref[...], k_ref[...], (((1,), (1,)), ((), ())), preferred_element_type=F32) * scale
    m = jnp.max(s, axis=-1, keepdims=True)
    p = jnp.exp(s - m)
    l = jnp.sum(p, axis=-1, keepdims=True)
    o = jnp.dot(p.astype(BF16), v_ref[...], preferred_element_type=F32) / l
    o_ref[...] = o.astype(o_ref.dtype)


def cross_attention(q, kv, *, bm=1024):
    s, xw = q.shape
    mem_len = kv.shape[0]
    dh = xw // X_HEADS
    bm = _tile(s, bm)
    return pl.pallas_call(
        functools.partial(_xattn_kernel, scale=dh ** -0.5),
        grid=(s // bm, X_HEADS),
        in_specs=[pl.BlockSpec((bm, dh), lambda i, h: (i, h)),
                  pl.BlockSpec((mem_len, dh), lambda i, h: (0, h)),
                  pl.BlockSpec((mem_len, dh), lambda i, h: (0, X_HEADS + h))],
        out_specs=pl.BlockSpec((bm, dh), lambda i, h: (i, h)),
        out_shape=jax.ShapeDtypeStruct((s, xw), BF16),
        compiler_params=_params(("parallel", "arbitrary"), 4 * bm * dh * 2, 4 * mem_len * dh * 2,
                                4 * bm * mem_len * 4),
        name="cross_attention",
    )(q, kv, kv)


def kernel(x, mem, w_in, w_conv, rel_bias, w_pool, pool_scale, w_out, w_xq, w_xkv, w_xo, w_gate, w_up, w_down,
           g_mix_pre, g_mix_post, g_x_pre, g_x_post, g_ffn_pre, g_ffn_post, g_mem):
    bsz, s, d = x.shape
    assert bsz == 1
    depth = w_in.shape[0]
    conv_w = w_conv.shape[2]
    pool_w = pool_scale.shape[1]
    att_w = w_out.shape[1] - conv_w - pool_w
    heads = att_w // ATT_HEAD_DIM
    ff_pad = -w_gate.shape[2] % FF_ALIGN

    w_in, w_pool, w_out, w_xq, w_xkv, w_xo = (w.astype(BF16) for w in (w_in, w_pool, w_out, w_xq, w_xkv, w_xo))
    w_gate = jnp.pad(w_gate.astype(BF16), ((0, 0), (0, 0), (0, ff_pad)))
    w_up = jnp.pad(w_up.astype(BF16), ((0, 0), (0, 0), (0, ff_pad)))
    w_down = jnp.pad(w_down.astype(BF16), ((0, 0), (0, ff_pad), (0, 0)))

    xs = x[0]
    memn = rmsnorm_bf16(mem[0], g_mem)
    xn = rmsnorm_bf16(xs, g_mix_pre[0])
    for l in range(depth):
        proj = matmul(xn, w_in[l], name="in_proj")
        y_conv = conv_mixer(proj, w_conv[l], conv_w)
        q_col = 3 * conv_w // ATT_HEAD_DIM
        y_att = chunk_attention(proj, rel_bias[l], q_col, q_col + heads, q_col + 2 * heads, heads)
        y_pool = pool_mixer(proj, w_pool[l], pool_scale[l], (3 * conv_w + 3 * att_w) // pool_w)
        xs, xn = matmul_epilogue([y_conv, y_att, y_pool], w_out[l], xs, g_mix_post[l], g_x_pre[l], bm=512)
        q = matmul(xn, w_xq[l], name="xq_proj")
        kv = matmul(memn, w_xkv[l], name="xkv_proj")
        o = cross_attention(q, kv)
        xs, xn = matmul_epilogue([o], w_xo[l], xs, g_x_post[l], g_ffn_pre[l], bm=512, bk=256)
        hid = gate_up(xn, w_gate[l], w_up[l])
        last = l == depth - 1
        g_next = g_ffn_post[l] if last else g_mix_pre[l + 1]
        xs, xn = matmul_epilogue([hid], w_down[l], xs, g_ffn_post[l], g_next, emit_xn=not last)
    return xs[None]
```

```python
import functools

import jax
import jax.numpy as jnp
import numpy as np
from jax import lax
from jax.experimental import pallas as pl
from jax.experimental.pallas import tpu as pltpu

CHUNK = 64
LEFT_CHUNKS = 8
MAX_REL = 256
POOL_WINDOWS = (2, 4, 8, 16)
CONV_KSIZE = 3
ATT_HEAD_DIM = 128
X_HEADS = 4
RMS_EPS = 1e-6
NEG_INF = -1e30

V7X_VMEM_BYTES = 64 * 1024 * 1024
V7X_LANES = 128
V7X_BF16_SUBLANES = 16

ATT_TQ = LEFT_CHUNKS * CHUNK
ATT_SUB = 2 * CHUNK
ATT_WIN = ATT_SUB + LEFT_CHUNKS * CHUNK
ATT_HEADS_PER_STEP = 4
HALO = V7X_BF16_SUBLANES
EPI_ROWS = V7X_BF16_SUBLANES
EPI_COLS = 1024
FF_ALIGN = 1024

BF16 = jnp.bfloat16
F32 = jnp.float32


def _vmem_limit(*buffer_bytes):
    need = int(sum(buffer_bytes)) + 8 * 1024 * 1024
    return min(need, V7X_VMEM_BYTES - 4 * 1024 * 1024)


def _params(semantics, *buffer_bytes):
    return pltpu.CompilerParams(dimension_semantics=semantics, vmem_limit_bytes=_vmem_limit(*buffer_bytes))


def _tile(n, pref):
    if n <= pref:
        return n
    t = pref
    while n % t:
        t -= V7X_LANES
    assert t > 0
    return t


def _rmsnorm_kernel(x_ref, g_ref, o_ref):
    x = x_ref[...]
    inv = lax.rsqrt(jnp.mean(x * x, axis=-1, keepdims=True) + RMS_EPS)
    o_ref[...] = (x * inv * g_ref[...]).astype(o_ref.dtype)


def rmsnorm_bf16(x, g):
    rows, d = x.shape
    br = _tile(rows, 256)
    return pl.pallas_call(
        _rmsnorm_kernel,
        grid=(rows // br,),
        in_specs=[pl.BlockSpec((br, d), lambda i: (i, 0)), pl.BlockSpec((1, d), lambda i: (0, 0))],
        out_specs=pl.BlockSpec((br, d), lambda i: (i, 0)),
        out_shape=jax.ShapeDtypeStruct((rows, d), BF16),
        compiler_params=_params(("parallel",), 2 * br * d * 4, 2 * br * d * 2),
        name="rmsnorm",
    )(x, g.reshape(1, d))


def _mm_kernel(a_ref, w_ref, o_ref):
    o_ref[...] = jnp.dot(a_ref[...], w_ref[...], preferred_element_type=F32).astype(o_ref.dtype)


def matmul(a, w, layer, *, bm=1024, bn=1024, name="matmul"):
    m, k = a.shape
    n = w.shape[2]
    bm, bn = _tile(m, bm), _tile(n, bn)
    return pl.pallas_call(
        _mm_kernel,
        grid=(m // bm, n // bn),
        in_specs=[pl.BlockSpec((bm, k), lambda i, j: (i, 0)),
                  pl.BlockSpec((None, k, bn), lambda i, j: (layer, 0, j))],
        out_specs=pl.BlockSpec((bm, bn), lambda i, j: (i, j)),
        out_shape=jax.ShapeDtypeStruct((m, n), BF16),
        compiler_params=_params(("parallel", "arbitrary"), 2 * bm * k * 2, 2 * k * bn * 2, 2 * bm * bn * 2,
                                bm * bn * 4),
        name=name,
    )(a, w)


def _gateup_kernel(a_ref, wg_ref, wu_ref, o_ref):
    a = a_ref[...]
    g = jnp.dot(a, wg_ref[...], preferred_element_type=F32)
    u = jnp.dot(a, wu_ref[...], preferred_element_type=F32)
    o_ref[...] = (g * jax.nn.sigmoid(g) * u).astype(o_ref.dtype)


def gate_up(a, wg, wu, layer, *, bm=1024, bn=512):
    m, k = a.shape
    n = wg.shape[2]
    bm, bn = _tile(m, bm), _tile(n, bn)
    return pl.pallas_call(
        _gateup_kernel,
        grid=(m // bm, n // bn),
        in_specs=[pl.BlockSpec((bm, k), lambda i, j: (i, 0)),
                  pl.BlockSpec((None, k, bn), lambda i, j: (layer, 0, j)),
                  pl.BlockSpec((None, k, bn), lambda i, j: (layer, 0, j))],
        out_specs=pl.BlockSpec((bm, bn), lambda i, j: (i, j)),
        out_shape=jax.ShapeDtypeStruct((m, n), BF16),
        compiler_params=_params(("parallel", "arbitrary"), 2 * bm * k * 2, 4 * k * bn * 2, 2 * bm * bn * 2,
                                3 * bm * bn * 4),
        name="gate_up",
    )(a, wg, wu)


def _mm_epilogue_kernel(*refs, seg_steps, nb, nk, nc, rc, emit_xn):
    nseg = len(seg_steps)
    a_refs = refs[:nseg]
    w_ref, x_ref, gpost_ref, gnext_ref, xo_ref = refs[nseg:nseg + 5]
    xn_ref = refs[nseg + 5] if emit_xn else None
    acc_ref = refs[-1]
    i = pl.program_id(0)
    k = pl.program_id(1)
    slot = lax.rem(i, 2)
    d = xo_ref.shape[1]

    def accumulate(a_ref):
        a = a_ref[...]
        for c0 in range(0, d, EPI_COLS):
            cols = slice(c0, c0 + EPI_COLS)
            acc_ref[slot, :, cols] += jnp.dot(a, w_ref[:, cols], preferred_element_type=F32)

    def finish_chunk():
        gpost = gpost_ref[...]
        gnext = gnext_ref[...]
        for r0 in range(0, rc, EPI_ROWS):
            rows = pl.ds(pl.multiple_of(k * rc + r0, EPI_ROWS), EPI_ROWS)
            out_rows = slice(r0, r0 + EPI_ROWS)
            m = acc_ref[1 - slot, rows, :]
            acc_ref[1 - slot, rows, :] = jnp.zeros_like(m)
            y = m * lax.rsqrt(jnp.mean(m * m, axis=-1, keepdims=True) + RMS_EPS) * gpost
            xnew = x_ref[out_rows, :] + y
            xo_ref[out_rows, :] = xnew
            if emit_xn:
                inv = lax.rsqrt(jnp.mean(xnew * xnew, axis=-1, keepdims=True) + RMS_EPS)
                xn_ref[out_rows, :] = (xnew * inv * gnext).astype(BF16)

    @pl.when((i == 0) & (k == 0))
    def _():
        acc_ref[...] = jnp.zeros(acc_ref.shape, F32)

    has_chunk = (i > 0) & (k < nc)
    if nseg == 1:
        @pl.when((i < nb) & has_chunk)
        def _():
            finish_chunk()
            accumulate(a_refs[0])

        @pl.when((i < nb) & jnp.logical_not(has_chunk))
        def _():
            accumulate(a_refs[0])

        @pl.when((i == nb) & has_chunk)
        def _():
            finish_chunk()
    else:
        k0 = 0
        for a_ref, steps in zip(a_refs, seg_steps):
            pl.when((i < nb) & (k >= k0) & (k < k0 + steps))(functools.partial(accumulate, a_ref))
            k0 += steps
        pl.when(has_chunk)(finish_chunk)


def matmul_epilogue(a_segs, w, layer, x, gpost, gnext, *, emit_xn=True, bm=1024, bk=512, rc=64):
    m, d = x.shape
    bm = _tile(m, bm)
    kdim = w.shape[1]
    bk = min([bk] + [a.shape[1] for a in a_segs])
    seg_steps = tuple(a.shape[1] // bk for a in a_segs)
    assert all(a.shape[1] % bk == 0 for a in a_segs) and sum(a.shape[1] for a in a_segs) == kdim
    assert d % EPI_COLS == 0 or d < EPI_COLS
    nk = kdim // bk
    nb = m // bm
    while bm % rc or bm // rc > nk:
        rc += EPI_ROWS
    nc = bm // rc

    def seg_spec(k0, steps):
        def imap(i, k):
            kk = jnp.where(i < nb, jnp.clip(k - k0, 0, steps - 1), steps - 1)
            return (jnp.minimum(i, nb - 1), kk)
        return pl.BlockSpec((bm, bk), imap)

    seg_specs, k0 = [], 0
    for steps in seg_steps:
        seg_specs.append(seg_spec(k0, steps))
        k0 += steps

    def chunk_map(i, k):
        return (jnp.where(i == 0, 0, (i - 1) * nc + jnp.minimum(k, nc - 1)), 0)

    chunk_spec = pl.BlockSpec((rc, d), chunk_map)
    vec_spec = pl.BlockSpec((1, d), lambda i, k: (0, 0))
    out_shape = [jax.ShapeDtypeStruct((m, d), F32)]
    out_specs = [chunk_spec]
    if emit_xn:
        out_shape.append(jax.ShapeDtypeStruct((m, d), BF16))
        out_specs.append(chunk_spec)
    res = pl.pallas_call(
        functools.partial(_mm_epilogue_kernel, seg_steps=seg_steps, nb=nb, nk=nk, nc=nc, rc=rc, emit_xn=emit_xn),
        grid=(nb + 1, nk),
        in_specs=seg_specs + [pl.BlockSpec((None, bk, d), lambda i, k: (layer, jnp.where(i < nb, k, nk - 1), 0)),
                              chunk_spec, vec_spec, vec_spec],
        out_specs=out_specs,
        out_shape=out_shape,
        scratch_shapes=[pltpu.VMEM((2, bm, d), F32)],
        compiler_params=_params(("arbitrary", "arbitrary"), 2 * bm * d * 4, 2 * len(a_segs) * bm * bk * 2,
                                2 * bk * d * 2, 4 * rc * d * 4, 2 * rc * d * 2, bm * min(d, EPI_COLS) * 4),
        name="matmul_epilogue",
    )(*a_segs, w, x, gpost.reshape(1, d), gnext.reshape(1, d))
    return (res[0], res[1]) if emit_xn else (res[0], None)


def _conv_kernel(b_ref, c_ref, h_ref, chalo_ref, hhalo_ref, w_ref, o_ref):
    i = pl.program_id(0)
    u = c_ref[...].astype(F32) * h_ref[...].astype(F32)
    uh = chalo_ref[...].astype(F32) * hhalo_ref[...].astype(F32)
    uh = jnp.where(i > 0, uh, 0.0)
    ext = jnp.concatenate([uh, u], axis=0)
    w = w_ref[...]
    conv = pltpu.roll(ext, CONV_KSIZE - 1, 0) * w[0:1]
    for tap in range(1, CONV_KSIZE):
        shift = CONV_KSIZE - 1 - tap
        shifted = pltpu.roll(ext, shift, 0) if shift else ext
        conv = conv + shifted * w[tap:tap + 1]
    o_ref[...] = (b_ref[...].astype(F32) * conv[HALO:]).astype(o_ref.dtype)


def conv_mixer(proj, w_conv, width, *, bm=512):
    s = proj.shape[0]
    bm = _tile(s, bm)
    hb = bm // HALO
    blk = lambda col: pl.BlockSpec((bm, width), lambda i: (i, col))
    halo = lambda col: pl.BlockSpec((HALO, width), lambda i: (jnp.maximum(i * hb - 1, 0), col))
    return pl.pallas_call(
        _conv_kernel,
        grid=(s // bm,),
        in_specs=[blk(0), blk(1), blk(2), halo(1), halo(2), pl.BlockSpec((CONV_KSIZE, width), lambda i: (0, 0))],
        out_specs=pl.BlockSpec((bm, width), lambda i: (i, 0)),
        out_shape=jax.ShapeDtypeStruct((s, width), BF16),
        compiler_params=_params(("parallel",), 8 * bm * width * 2, 6 * bm * width * 4),
        name="conv_mixer",
    )(proj, proj, proj, proj, proj, w_conv)


def _pool_kernel(u_ref, uhalo_ref, w_ref, scale_ref, o_ref, *, gw):
    i = pl.program_id(0)
    bm = u_ref.shape[0]
    t = i * bm + lax.broadcasted_iota(jnp.int32, (bm, 1), 0)
    for g, win in enumerate(POOL_WINDOWS):
        cols = slice(g * gw, (g + 1) * gw)
        u = u_ref[:, cols].astype(F32)
        uh = jnp.where(i > 0, uhalo_ref[:, cols].astype(F32), 0.0)
        acc = jnp.concatenate([uh, u], axis=0)
        span = 1
        while span < win:
            acc = acc + pltpu.roll(acc, span, 0)
            span *= 2
        count = jnp.minimum(t + 1, win).astype(F32)
        pooled = (acc[HALO:] / count - u).astype(BF16)
        mixed = jnp.dot(pooled, w_ref[g], preferred_element_type=F32)
        o_ref[:, cols] = (mixed * scale_ref[:, cols]).astype(o_ref.dtype)


def pool_mixer(proj, w_pool, pool_scale, col_block, *, bm=512):
    s = proj.shape[0]
    groups, gw, _ = w_pool.shape
    width = groups * gw
    bm = _tile(s, bm)
    hb = bm // HALO
    return pl.pallas_call(
        functools.partial(_pool_kernel, gw=gw),
        grid=(s // bm,),
        in_specs=[pl.BlockSpec((bm, width), lambda i: (i, col_block)),
                  pl.BlockSpec((HALO, width), lambda i: (jnp.maximum(i * hb - 1, 0), col_block)),
                  pl.BlockSpec((groups, gw, gw), lambda i: (0, 0, 0)),
                  pl.BlockSpec((1, width), lambda i: (0, 0))],
        out_specs=pl.BlockSpec((bm, width), lambda i: (i, 0)),
        out_shape=jax.ShapeDtypeStruct((s, width), BF16),
        compiler_params=_params(("parallel",), 4 * bm * width * 2, 6 * bm * width * 4),
        name="pool_mixer",
    )(proj, proj, w_pool, pool_scale.reshape(1, width))


def _attn_kernel(q_ref, kp_ref, kc_ref, vp_ref, vc_ref, bvec_ref, o_ref, bias_ref, *, scale):
    qi = pl.program_id(1)
    dh = ATT_HEAD_DIM
    shape = (ATT_SUB, ATT_WIN)

    @pl.when(qi == 0)
    def _():
        qc = lax.broadcasted_iota(jnp.int32, shape, 0) // CHUNK
        kc = lax.broadcasted_iota(jnp.int32, shape, 1) // CHUNK
        band = (kc >= qc) & (kc <= qc + LEFT_CHUNKS)
        for hh in range(ATT_HEADS_PER_STEP):
            toeplitz = pltpu.roll(jnp.broadcast_to(bvec_ref[hh], shape), 0, 1, stride=1, stride_axis=0)
            bias_ref[hh] = jnp.where(band, toeplitz, NEG_INF)

    def attend(first_block):
        for hh in range(ATT_HEADS_PER_STEP):
            cols = slice(hh * dh, (hh + 1) * dh)
            for sub in range(ATT_TQ // ATT_SUB):
                lo, hi = sub * ATT_SUB, (sub + 1) * ATT_SUB
                k = jnp.concatenate([kp_ref[lo:, cols], kc_ref[:hi, cols]], axis=0)
                v = jnp.concatenate([vp_ref[lo:, cols], vc_ref[:hi, cols]], axis=0)
                s = lax.dot_general(q_ref[lo:hi, cols], k, (((1,), (1,)), ((), ())), preferred_element_type=F32)
                s = s * scale + bias_ref[hh]
                if first_block:
                    col = lax.broadcasted_iota(jnp.int32, shape, 1)
                    s = jnp.where(col >= ATT_TQ - lo, s, NEG_INF)
                m = jnp.max(s, axis=-1, keepdims=True)
                p = jnp.exp(s - m)
                l = jnp.sum(p, axis=-1, keepdims=True)
                o = jnp.dot(p.astype(BF16), v, preferred_element_type=F32) / l
                o_ref[lo:hi, cols] = o.astype(o_ref.dtype)

    pl.when(qi == 0)(functools.partial(attend, True))
    pl.when(qi > 0)(functools.partial(attend, False))


def _bias_by_offset(rel_bias):
    heads = rel_bias.shape[0]
    far = rel_bias[:, 2 * MAX_REL:]
    near = rel_bias[:, MAX_REL - CHUNK + 1:][:, ::-1]
    head = jnp.broadcast_to(far, (heads, ATT_TQ - MAX_REL))
    tail = jnp.broadcast_to(far, (heads, ATT_WIN - ATT_TQ - CHUNK))
    return jnp.concatenate([head, near, tail], axis=1)[:, None, :].astype(F32)


def chunk_attention(proj, rel_bias, q_col, k_col, v_col, heads):
    s = proj.shape[0]
    hg = ATT_HEADS_PER_STEP
    assert s % ATT_TQ == 0 and ATT_TQ >= MAX_REL and heads % hg == 0
    assert q_col % hg == 0 and k_col % hg == 0 and v_col % hg == 0
    width = hg * ATT_HEAD_DIM
    cur = lambda col: pl.BlockSpec((ATT_TQ, width), lambda g, i: (i, col // hg + g))
    prev = lambda col: pl.BlockSpec((ATT_TQ, width), lambda g, i: (jnp.maximum(i - 1, 0), col // hg + g))
    return pl.pallas_call(
        functools.partial(_attn_kernel, scale=ATT_HEAD_DIM ** -0.5),
        grid=(heads // hg, s // ATT_TQ),
        in_specs=[cur(q_col), prev(k_col), cur(k_col), prev(v_col), cur(v_col),
                  pl.BlockSpec((hg, 1, ATT_WIN), lambda g, i: (g, 0, 0))],
        out_specs=pl.BlockSpec((ATT_TQ, width), lambda g, i: (i, g)),
        out_shape=jax.ShapeDtypeStruct((s, heads * ATT_HEAD_DIM), BF16),
        scratch_shapes=[pltpu.VMEM((hg, ATT_SUB, ATT_WIN), F32)],
        compiler_params=_params(("parallel", "arbitrary"), 12 * ATT_TQ * width * 2, hg * ATT_SUB * ATT_WIN * 4,
                                8 * ATT_SUB * ATT_WIN * 4),
        name="chunk_attention",
    )(proj, proj, proj, proj, proj, _bias_by_offset(rel_bias))


def _xattn_kernel(q_ref, k_ref, v_ref, o_ref, *, scale):
    s = lax.dot_general(q_ref[...], k_ref[...], (((1,), (1,)), ((), ())), preferred_element_type=F32) * scale
    m = jnp.max(s, axis=-1, keepdims=True)
    p = jnp.exp(s - m)
    l = jnp.sum(p, axis=-1, keepdims=True)
    o = jnp.dot(p.astype(BF16), v_ref[...], preferred_element_type=F32) / l
    o_ref[...] = o.astype(o_ref.dtype)


def cross_attention(q, kv, *, bm=1024):
    s, xw = q.shape
    mem_len = kv.shape[0]
    dh = xw // X_HEADS
    bm = _tile(s, bm)
    return pl.pallas_call(
        functools.partial(_xattn_kernel, scale=dh ** -0.5),
        grid=(s // bm, X_HEADS),
        in_specs=[pl.BlockSpec((bm, dh), lambda i, h: (i, h)),
                  pl.BlockSpec((mem_len, dh), lambda i, h: (0, h)),
                  pl.BlockSpec((mem_len, dh), lambda i, h: (0, X_HEADS + h))],
        out_specs=pl.BlockSpec((bm, dh), lambda i, h: (i, h)),
        out_shape=jax.ShapeDtypeStruct((s, xw), BF16),
        compiler_params=_params(("parallel", "arbitrary"), 4 * bm * dh * 2, 4 * mem_len * dh * 2,
                                4 * bm * mem_len * 4),
        name="cross_attention",
    )(q, kv, kv)


def kernel(x, mem, w_in, w_conv, rel_bias, w_pool, pool_scale, w_out, w_xq, w_xkv, w_xo, w_gate, w_up, w_down,
           g_mix_pre, g_mix_post, g_x_pre, g_x_post, g_ffn_pre, g_ffn_post, g_mem):
    bsz, s, d = x.shape
    assert bsz == 1
    depth = w_in.shape[0]
    conv_w = w_conv.shape[2]
    pool_w = pool_scale.shape[1]
    att_w = w_out.shape[1] - conv_w - pool_w
    heads = att_w // ATT_HEAD_DIM
    ff_pad = -w_gate.shape[2] % FF_ALIGN

    w_in, w_pool, w_out, w_xq, w_xkv, w_xo = (w.astype(BF16) for w in (w_in, w_pool, w_out, w_xq, w_xkv, w_xo))
    w_gate = jnp.pad(w_gate.astype(BF16), ((0, 0), (0, 0), (0, ff_pad)))
    w_up = jnp.pad(w_up.astype(BF16), ((0, 0), (0, 0), (0, ff_pad)))
    w_down = jnp.pad(w_down.astype(BF16), ((0, 0), (0, ff_pad), (0, 0)))

    xs = x[0]
    memn = rmsnorm_bf16(mem[0], g_mem)
    xn = rmsnorm_bf16(xs, g_mix_pre[0])
    for l in range(depth):
        proj = matmul(xn, w_in, l, name="in_proj")
        y_conv = conv_mixer(proj, w_conv[l], conv_w)
        q_col = 3 * conv_w // ATT_HEAD_DIM
        y_att = chunk_attention(proj, rel_bias[l], q_col, q_col + heads, q_col + 2 * heads, heads)
        y_pool = pool_mixer(proj, w_pool[l], pool_scale[l], (3 * conv_w + 3 * att_w) // pool_w)
        xs, xn = matmul_epilogue([y_conv, y_att, y_pool], w_out, l, xs, g_mix_post[l], g_x_pre[l], bm=512)
        q = matmul(xn, w_xq, l, name="xq_proj")
        kv = matmul(memn, w_xkv, l, name="xkv_proj")
        o = cross_attention(q, kv)
        xs, xn = matmul_epilogue([o], w_xo, l, xs, g_x_post[l], g_ffn_pre[l], bm=512, bk=256)
        hid = gate_up(xn, w_gate, w_up, l)
        last = l == depth - 1
        g_next = g_ffn_post[l] if last else g_mix_pre[l + 1]
        xs, xn = matmul_epilogue([hid], w_down, l, xs, g_ffn_post[l], g_next, emit_xn=not last)
    return xs[None]
```

```python
import functools

import jax
import jax.numpy as jnp
import numpy as np
from jax import lax
from jax.experimental import pallas as pl
from jax.experimental.pallas import tpu as pltpu

CHUNK = 64
LEFT_CHUNKS = 8
MAX_REL = 256
POOL_WINDOWS = (2, 4, 8, 16)
CONV_KSIZE = 3
ATT_HEAD_DIM = 128
X_HEADS = 4
RMS_EPS = 1e-6
NEG_INF = -1e30

V7X_VMEM_BYTES = 64 * 1024 * 1024
V7X_LANES = 128
V7X_BF16_SUBLANES = 16

ATT_TQ = LEFT_CHUNKS * CHUNK
ATT_SUB = 2 * CHUNK
ATT_WIN = ATT_SUB + LEFT_CHUNKS * CHUNK
ATT_HEADS_PER_STEP = 4
HALO = V7X_BF16_SUBLANES
EPI_ROWS = V7X_BF16_SUBLANES
EPI_COLS = 1024
LOG2E = 1.4426950408889634

BF16 = jnp.bfloat16
F32 = jnp.float32


def _vmem_limit(*buffer_bytes):
    need = int(sum(buffer_bytes)) + 8 * 1024 * 1024
    return min(need, V7X_VMEM_BYTES - 4 * 1024 * 1024)


def _params(semantics, *buffer_bytes):
    return pltpu.CompilerParams(dimension_semantics=semantics, vmem_limit_bytes=_vmem_limit(*buffer_bytes))


def _tile(n, pref):
    if n <= pref:
        return n
    t = pref
    while n % t:
        t -= V7X_LANES
    assert t > 0
    return t


def _rmsnorm_kernel(x_ref, g_ref, o_ref):
    x = x_ref[...]
    inv = lax.rsqrt(jnp.mean(x * x, axis=-1, keepdims=True) + RMS_EPS)
    o_ref[...] = (x * inv * g_ref[...]).astype(o_ref.dtype)


def rmsnorm_bf16(x, g):
    rows, d = x.shape
    br = _tile(rows, 256)
    return pl.pallas_call(
        _rmsnorm_kernel,
        grid=(rows // br,),
        in_specs=[pl.BlockSpec((br, d), lambda i: (i, 0)), pl.BlockSpec((1, d), lambda i: (0, 0))],
        out_specs=pl.BlockSpec((br, d), lambda i: (i, 0)),
        out_shape=jax.ShapeDtypeStruct((rows, d), BF16),
        compiler_params=_params(("parallel",), 2 * br * d * 4, 2 * br * d * 2),
        name="rmsnorm",
    )(x, g.reshape(1, d))


def _mm_kernel(a_ref, w_ref, o_ref):
    o_ref[...] = jnp.dot(a_ref[...], w_ref[...], preferred_element_type=F32).astype(o_ref.dtype)


def matmul(a, w, layer, *, bm=1024, bn=1024, name="matmul"):
    m, k = a.shape
    n = w.shape[2]
    bm, bn = _tile(m, bm), _tile(n, bn)
    return pl.pallas_call(
        _mm_kernel,
        grid=(m // bm, n // bn),
        in_specs=[pl.BlockSpec((bm, k), lambda i, j: (i, 0)),
                  pl.BlockSpec((None, k, bn), lambda i, j: (layer, 0, j))],
        out_specs=pl.BlockSpec((bm, bn), lambda i, j: (i, j)),
        out_shape=jax.ShapeDtypeStruct((m, n), BF16),
        compiler_params=_params(("parallel", "arbitrary"), 2 * bm * k * 2, 2 * k * bn * 2, 2 * bm * bn * 2,
                                bm * bn * 4),
        name=name,
    )(a, w)


def _gateup_kernel(a_ref, wg_ref, wu_ref, o_ref):
    a = a_ref[...]
    g = jnp.dot(a, wg_ref[...], preferred_element_type=F32)
    u = jnp.dot(a, wu_ref[...], preferred_element_type=F32)
    o_ref[...] = (g * jax.nn.sigmoid(g) * u).astype(o_ref.dtype)


def gate_up(a, wg, wu, layer, *, bm=1024, bn=512):
    m, k = a.shape
    n = wg.shape[2]
    bm, bn = _tile(m, bm), min(bn, n)
    return pl.pallas_call(
        _gateup_kernel,
        grid=(m // bm, pl.cdiv(n, bn)),
        in_specs=[pl.BlockSpec((bm, k), lambda i, j: (i, 0)),
                  pl.BlockSpec((None, k, bn), lambda i, j: (layer, 0, j)),
                  pl.BlockSpec((None, k, bn), lambda i, j: (layer, 0, j))],
        out_specs=pl.BlockSpec((bm, bn), lambda i, j: (i, j)),
        out_shape=jax.ShapeDtypeStruct((m, n), BF16),
        compiler_params=_params(("parallel", "arbitrary"), 2 * bm * k * 2, 4 * k * bn * 2, 2 * bm * bn * 2,
                                3 * bm * bn * 4),
        name="gate_up",
    )(a, wg, wu)


def _mm_epilogue_kernel(*refs, seg_steps, nb, nk, nc, rc, k_tail, merge, emit_xn):
    nseg = len(seg_steps)
    a_refs = refs[:nseg]
    w_ref, x_ref, gpost_ref, gnext_ref, xo_ref = refs[nseg:nseg + 5]
    xn_ref = refs[nseg + 5] if emit_xn else None
    acc_ref = refs[-1]
    i = pl.program_id(0)
    k = pl.program_id(1)
    slot = lax.rem(i, 2)
    d = xo_ref.shape[1]
    bk = w_ref.shape[0]

    def accumulate(a_ref, kk=bk):
        a = a_ref[:, :kk]
        for c0 in range(0, d, EPI_COLS):
            cols = slice(c0, c0 + EPI_COLS)
            acc_ref[slot, :, cols] += jnp.dot(a, w_ref[:kk, cols], preferred_element_type=F32)

    def finish_chunk():
        gpost = gpost_ref[...]
        gnext = gnext_ref[...]
        for r0 in range(0, rc, EPI_ROWS):
            rows = pl.ds(pl.multiple_of(k * rc + r0, EPI_ROWS), EPI_ROWS)
            out_rows = slice(r0, r0 + EPI_ROWS)
            m = acc_ref[1 - slot, rows, :]
            acc_ref[1 - slot, rows, :] = jnp.zeros_like(m)
            y = m * lax.rsqrt(jnp.mean(m * m, axis=-1, keepdims=True) + RMS_EPS) * gpost
            xnew = x_ref[out_rows, :] + y
            xo_ref[out_rows, :] = xnew
            if emit_xn:
                inv = lax.rsqrt(jnp.mean(xnew * xnew, axis=-1, keepdims=True) + RMS_EPS)
                xn_ref[out_rows, :] = (xnew * inv * gnext).astype(BF16)

    @pl.when((i == 0) & (k == 0))
    def _():
        acc_ref[...] = jnp.zeros(acc_ref.shape, F32)

    has_chunk = (i > 0) & (k < nc)
    if merge:
        @pl.when((i < nb) & has_chunk)
        def _():
            finish_chunk()
            accumulate(a_refs[0])

        full = k < nk - 1 if k_tail != bk else True
        @pl.when((i < nb) & jnp.logical_not(has_chunk) & full)
        def _():
            accumulate(a_refs[0])

        if k_tail != bk:
            @pl.when((i < nb) & jnp.logical_not(has_chunk) & (k == nk - 1))
            def _():
                accumulate(a_refs[0], k_tail)

        @pl.when((i == nb) & has_chunk)
        def _():
            finish_chunk()
    else:
        k0 = 0
        for a_ref, steps in zip(a_refs, seg_steps):
            pl.when((i < nb) & (k >= k0) & (k < k0 + steps))(functools.partial(accumulate, a_ref))
            k0 += steps
        pl.when(has_chunk)(finish_chunk)


def matmul_epilogue(a_segs, w, layer, x, gpost, gnext, *, emit_xn=True, merge=False, bm=1024, bk=512, rc=64):
    m, d = x.shape
    bm = _tile(m, bm)
    kdim = w.shape[1]
    bk = min([bk] + [a.shape[1] for a in a_segs])
    seg_steps = tuple(pl.cdiv(a.shape[1], bk) for a in a_segs)
    assert sum(a.shape[1] for a in a_segs) == kdim and (merge or kdim % bk == 0) and not (merge and len(a_segs) > 1)
    assert d % EPI_COLS == 0 or d < EPI_COLS
    nk = sum(seg_steps)
    k_tail = kdim - (nk - 1) * bk
    nb = m // bm
    most = nk if k_tail == bk else nk - 1
    while bm % rc or bm // rc > most:
        rc += EPI_ROWS
    nc = bm // rc

    def seg_spec(k0, steps):
        def imap(i, k):
            kk = jnp.where(i < nb, jnp.clip(k - k0, 0, steps - 1), steps - 1)
            return (jnp.minimum(i, nb - 1), kk)
        return pl.BlockSpec((bm, bk), imap)

    seg_specs, k0 = [], 0
    for steps in seg_steps:
        seg_specs.append(seg_spec(k0, steps))
        k0 += steps

    def chunk_map(i, k):
        return (jnp.where(i == 0, 0, (i - 1) * nc + jnp.minimum(k, nc - 1)), 0)

    chunk_spec = pl.BlockSpec((rc, d), chunk_map)
    vec_spec = pl.BlockSpec((1, d), lambda i, k: (0, 0))
    out_shape = [jax.ShapeDtypeStruct((m, d), F32)]
    out_specs = [chunk_spec]
    if emit_xn:
        out_shape.append(jax.ShapeDtypeStruct((m, d), BF16))
        out_specs.append(chunk_spec)
    res = pl.pallas_call(
        functools.partial(_mm_epilogue_kernel, seg_steps=seg_steps, nb=nb, nk=nk, nc=nc, rc=rc, k_tail=k_tail,
                          merge=merge, emit_xn=emit_xn),
        grid=(nb + 1, nk),
        in_specs=seg_specs + [pl.BlockSpec((None, bk, d), lambda i, k: (layer, jnp.where(i < nb, k, nk - 1), 0)),
                              chunk_spec, vec_spec, vec_spec],
        out_specs=out_specs,
        out_shape=out_shape,
        scratch_shapes=[pltpu.VMEM((2, bm, d), F32)],
        compiler_params=_params(("arbitrary", "arbitrary"), 2 * bm * d * 4, 2 * len(a_segs) * bm * bk * 2,
                                2 * bk * d * 2, 4 * rc * d * 4, 2 * rc * d * 2, bm * min(d, EPI_COLS) * 4),
        name="matmul_epilogue",
    )(*a_segs, w, x, gpost.reshape(1, d), gnext.reshape(1, d))
    return (res[0], res[1]) if emit_xn else (res[0], None)


def _conv_kernel(b_ref, c_ref, h_ref, chalo_ref, hhalo_ref, w_ref, o_ref):
    i = pl.program_id(0)
    u = c_ref[...].astype(F32) * h_ref[...].astype(F32)
    uh = chalo_ref[...].astype(F32) * hhalo_ref[...].astype(F32)
    uh = jnp.where(i > 0, uh, 0.0)
    ext = jnp.concatenate([uh, u], axis=0)
    w = w_ref[...]
    conv = pltpu.roll(ext, CONV_KSIZE - 1, 0) * w[0:1]
    for tap in range(1, CONV_KSIZE):
        shift = CONV_KSIZE - 1 - tap
        shifted = pltpu.roll(ext, shift, 0) if shift else ext
        conv = conv + shifted * w[tap:tap + 1]
    o_ref[...] = (b_ref[...].astype(F32) * conv[HALO:]).astype(o_ref.dtype)


def conv_mixer(proj, w_conv, width, *, bm=512):
    s = proj.shape[0]
    bm = _tile(s, bm)
    hb = bm // HALO
    blk = lambda col: pl.BlockSpec((bm, width), lambda i: (i, col))
    halo = lambda col: pl.BlockSpec((HALO, width), lambda i: (jnp.maximum(i * hb - 1, 0), col))
    return pl.pallas_call(
        _conv_kernel,
        grid=(s // bm,),
        in_specs=[blk(0), blk(1), blk(2), halo(1), halo(2), pl.BlockSpec((CONV_KSIZE, width), lambda i: (0, 0))],
        out_specs=pl.BlockSpec((bm, width), lambda i: (i, 0)),
        out_shape=jax.ShapeDtypeStruct((s, width), BF16),
        compiler_params=_params(("parallel",), 8 * bm * width * 2, 6 * bm * width * 4),
        name="conv_mixer",
    )(proj, proj, proj, proj, proj, w_conv)


def _pool_kernel(u_ref, uhalo_ref, w_ref, scale_ref, o_ref, *, gw):
    i = pl.program_id(0)
    bm = u_ref.shape[0]
    t = i * bm + lax.broadcasted_iota(jnp.int32, (bm, 1), 0)
    for g, win in enumerate(POOL_WINDOWS):
        cols = slice(g * gw, (g + 1) * gw)
        u = u_ref[:, cols].astype(F32)
        uh = jnp.where(i > 0, uhalo_ref[:, cols].astype(F32), 0.0)
        acc = jnp.concatenate([uh, u], axis=0)
        span = 1
        while span < win:
            acc = acc + pltpu.roll(acc, span, 0)
            span *= 2
        count = jnp.minimum(t + 1, win).astype(F32)
        pooled = (acc[HALO:] / count - u).astype(BF16)
        mixed = jnp.dot(pooled, w_ref[g], preferred_element_type=F32)
        o_ref[:, cols] = (mixed * scale_ref[:, cols]).astype(o_ref.dtype)


def pool_mixer(proj, w_pool, pool_scale, col_block, *, bm=512):
    s = proj.shape[0]
    groups, gw, _ = w_pool.shape
    width = groups * gw
    bm = _tile(s, bm)
    hb = bm // HALO
    return pl.pallas_call(
        functools.partial(_pool_kernel, gw=gw),
        grid=(s // bm,),
        in_specs=[pl.BlockSpec((bm, width), lambda i: (i, col_block)),
                  pl.BlockSpec((HALO, width), lambda i: (jnp.maximum(i * hb - 1, 0), col_block)),
                  pl.BlockSpec((groups, gw, gw), lambda i: (0, 0, 0)),
                  pl.BlockSpec((1, width), lambda i: (0, 0))],
        out_specs=pl.BlockSpec((bm, width), lambda i: (i, 0)),
        out_shape=jax.ShapeDtypeStruct((s, width), BF16),
        compiler_params=_params(("parallel",), 4 * bm * width * 2, 6 * bm * width * 4),
        name="pool_mixer",
    )(proj, proj, w_pool, pool_scale.reshape(1, width))


def _attn_kernel(q_ref, kp_ref, kc_ref, vp_ref, vc_ref, bvec_ref, o_ref, bias_ref, *, scale):
    qi = pl.program_id(1)
    dh = ATT_HEAD_DIM
    shape = (ATT_SUB, ATT_WIN)

    @pl.when(qi == 0)
    def _():
        qc = lax.broadcasted_iota(jnp.int32, shape, 0) // CHUNK
        kc = lax.broadcasted_iota(jnp.int32, shape, 1) // CHUNK
        band = (kc >= qc) & (kc <= qc + LEFT_CHUNKS)
        for hh in range(ATT_HEADS_PER_STEP):
            toeplitz = pltpu.roll(jnp.broadcast_to(bvec_ref[hh], shape), 0, 1, stride=1, stride_axis=0)
            bias_ref[hh] = jnp.where(band, toeplitz * LOG2E, NEG_INF)

    def attend(first_block):
        for hh in range(ATT_HEADS_PER_STEP):
            cols = slice(hh * dh, (hh + 1) * dh)
            for sub in range(ATT_TQ // ATT_SUB):
                lo, hi = sub * ATT_SUB, (sub + 1) * ATT_SUB
                k = jnp.concatenate([kp_ref[lo:, cols], kc_ref[:hi, cols]], axis=0)
                v = jnp.concatenate([vp_ref[lo:, cols], vc_ref[:hi, cols]], axis=0)
                s = lax.dot_general(q_ref[lo:hi, cols], k, (((1,), (1,)), ((), ())), preferred_element_type=F32)
                s = s * (scale * LOG2E) + bias_ref[hh]
                if first_block:
                    col = lax.broadcasted_iota(jnp.int32, shape, 1)
                    s = jnp.where(col >= ATT_TQ - lo, s, NEG_INF)
                m = jnp.max(s, axis=-1, keepdims=True)
                p = jnp.exp2(s - m)
                l = jnp.sum(p, axis=-1, keepdims=True)
                o = jnp.dot(p.astype(BF16), v, preferred_element_type=F32) / l
                o_ref[lo:hi, cols] = o.astype(o_ref.dtype)

    pl.when(qi == 0)(functools.partial(attend, True))
    pl.when(qi > 0)(functools.partial(attend, False))


def _bias_by_offset(rel_bias):
    heads = rel_bias.shape[0]
    far = rel_bias[:, 2 * MAX_REL:]
    near = rel_bias[:, MAX_REL - CHUNK + 1:][:, ::-1]
    head = jnp.broadcast_to(far, (heads, ATT_TQ - MAX_REL))
    tail = jnp.broadcast_to(far, (heads, ATT_WIN - ATT_TQ - CHUNK))
    return jnp.concatenate([head, near, tail], axis=1)[:, None, :].astype(F32)


def chunk_attention(proj, rel_bias, q_col, k_col, v_col, heads):
    s = proj.shape[0]
    hg = ATT_HEADS_PER_STEP
    assert s % ATT_TQ == 0 and ATT_TQ >= MAX_REL and heads % hg == 0
    assert q_col % hg == 0 and k_col % hg == 0 and v_col % hg == 0
    width = hg * ATT_HEAD_DIM
    cur = lambda col: pl.BlockSpec((ATT_TQ, width), lambda g, i: (i, col // hg + g))
    prev = lambda col: pl.BlockSpec((ATT_TQ, width), lambda g, i: (jnp.maximum(i - 1, 0), col // hg + g))
    return pl.pallas_call(
        functools.partial(_attn_kernel, scale=ATT_HEAD_DIM ** -0.5),
        grid=(heads // hg, s // ATT_TQ),
        in_specs=[cur(q_col), prev(k_col), cur(k_col), prev(v_col), cur(v_col),
                  pl.BlockSpec((hg, 1, ATT_WIN), lambda g, i: (g, 0, 0))],
        out_specs=pl.BlockSpec((ATT_TQ, width), lambda g, i: (i, g)),
        out_shape=jax.ShapeDtypeStruct((s, heads * ATT_HEAD_DIM), BF16),
        scratch_shapes=[pltpu.VMEM((hg, ATT_SUB, ATT_WIN), F32)],
        compiler_params=_params(("parallel", "arbitrary"), 12 * ATT_TQ * width * 2, hg * ATT_SUB * ATT_WIN * 4,
                                8 * ATT_SUB * ATT_WIN * 4),
        name="chunk_attention",
    )(proj, proj, proj, proj, proj, _bias_by_offset(rel_bias))


def _xattn_kernel(q_ref, k_ref, v_ref, o_ref, *, scale):
    s = lax.dot_general(q_ref[...], k_ref[...], (((1,), (1,)), ((), ())), preferred_element_type=F32) * scale
    m = jnp.max(s, axis=-1, keepdims=True)
    p = jnp.exp(s - m)
    l = jnp.sum(p, axis=-1, keepdims=True)
    o = jnp.dot(p.astype(BF16), v_ref[...], preferred_element_type=F32) / l
    o_ref[...] = o.astype(o_ref.dtype)


def cross_attention(q, kv, *, bm=1024):
    s, xw = q.shape
    mem_len = kv.shape[0]
    dh = xw // X_HEADS
    bm = _tile(s, bm)
    return pl.pallas_call(
        functools.partial(_xattn_kernel, scale=dh ** -0.5),
        grid=(s // bm, X_HEADS),
        in_specs=[pl.BlockSpec((bm, dh), lambda i, h: (i, h)),
                  pl.BlockSpec((mem_len, dh), lambda i, h: (0, h)),
                  pl.BlockSpec((mem_len, dh), lambda i, h: (0, X_HEADS + h))],
        out_specs=pl.BlockSpec((bm, dh), lambda i, h: (i, h)),
        out_shape=jax.ShapeDtypeStruct((s, xw), BF16),
        compiler_params=_params(("parallel", "arbitrary"), 4 * bm * dh * 2, 4 * mem_len * dh * 2,
                                4 * bm * mem_len * 4),
        name="cross_attention",
    )(q, kv, kv)


def kernel(x, mem, w_in, w_conv, rel_bias, w_pool, pool_scale, w_out, w_xq, w_xkv, w_xo, w_gate, w_up, w_down,
           g_mix_pre, g_mix_post, g_x_pre, g_x_post, g_ffn_pre, g_ffn_post, g_mem):
    bsz, s, d = x.shape
    assert bsz == 1
    depth = w_in.shape[0]
    conv_w = w_conv.shape[2]
    pool_w = pool_scale.shape[1]
    att_w = w_out.shape[1] - conv_w - pool_w
    heads = att_w // ATT_HEAD_DIM

    w_in, w_pool, w_out, w_xq, w_xkv, w_xo, w_gate, w_up, w_down = (
        w.astype(BF16) for w in (w_in, w_pool, w_out, w_xq, w_xkv, w_xo, w_gate, w_up, w_down))

    xs = x[0]
    memn = rmsnorm_bf16(mem[0], g_mem)
    xn = rmsnorm_bf16(xs, g_mix_pre[0])
    for l in range(depth):
        proj = matmul(xn, w_in, l, name="in_proj")
        y_conv = conv_mixer(proj, w_conv[l], conv_w)
        q_col = 3 * conv_w // ATT_HEAD_DIM
        y_att = chunk_attention(proj, rel_bias[l], q_col, q_col + heads, q_col + 2 * heads, heads)
        y_pool = pool_mixer(proj, w_pool[l], pool_scale[l], (3 * conv_w + 3 * att_w) // pool_w)
        xs, xn = matmul_epilogue([y_conv, y_att, y_pool], w_out, l, xs, g_mix_post[l], g_x_pre[l], bm=512)
        q = matmul(xn, w_xq, l, name="xq_proj")
        kv = matmul(memn, w_xkv, l, name="xkv_proj")
        o = cross_attention(q, kv)
        xs, xn = matmul_epilogue([o], w_xo, l, xs, g_x_post[l], g_ffn_pre[l], bm=512, bk=256)
        hid = gate_up(xn, w_gate, w_up, l)
        last = l == depth - 1
        g_next = g_ffn_post[l] if last else g_mix_pre[l + 1]
        xs, xn = matmul_epilogue([hid], w_down, l, xs, g_ffn_post[l], g_next, emit_xn=not last, merge=True)
    return xs[None]
```
